```python
import jax, jax.numpy as jnp
from jax import lax
import numpy as np

D_MODEL = 1024
BATCH = 1
SEQ = 16384
DEPTH = 1

RET_HEADS = 8
RET_DK = 64
RET_DV = 128
RET_CHUNK = 128
ROPE_BASE = 10000.0
SB_HEADS = 8
SB_DH = 64
SB_BLOCK = 128
PEER_HEADS = 8
N_KEYS = 128
N_EXPERTS = N_KEYS * N_KEYS
PEER_TOPK = 16
PEER_DKEY = 256
PEER_CHUNK = 128
EPS = 1e-6

RET_QK = RET_HEADS * RET_DK
RET_V = RET_HEADS * RET_DV
SB_W = SB_HEADS * SB_DH
IN_SIZES = (RET_QK, RET_QK, RET_V, RET_V, SB_W, SB_W, SB_W, D_MODEL, D_MODEL)
IN_WIDTH = sum(IN_SIZES)

kernel_name = 'hybrid_retention_stickbreak_peer'


def rmsnorm(x, g):
    xf = x.astype(jnp.float32)
    y = xf * lax.rsqrt(jnp.mean(xf * xf, axis=-1, keepdims=True) + EPS)
    return (y * g.astype(jnp.float32)).astype(x.dtype)


def group_norm(y, g):
    yf = y.astype(jnp.float32)
    mu = jnp.mean(yf, axis=-1, keepdims=True)
    var = jnp.mean(jnp.square(yf - mu), axis=-1, keepdims=True)
    return (yf - mu) * lax.rsqrt(var + EPS) * g.astype(jnp.float32)


def rope(x, pos):
    half = x.shape[-1] // 2
    freqs = ROPE_BASE ** (-jnp.arange(half, dtype=jnp.float32) / half)
    ang = pos[:, None] * freqs[None, :]
    cos = jnp.cos(ang)[:, None, :]
    sin = jnp.sin(ang)[:, None, :]
    xf = x.astype(jnp.float32)
    x1, x2 = xf[..., :half], xf[..., half:]
    return jnp.concatenate([x1 * cos - x2 * sin, x1 * sin + x2 * cos], axis=-1)


def retention(q, k, v):
    B, S, H, dk = q.shape
    dv = v.shape[-1]
    C = RET_CHUNK
    n = S // C
    log_g = jnp.log1p(-jnp.exp2(-5.0 - jnp.arange(H, dtype=jnp.float32)))
    i = jnp.arange(C, dtype=jnp.float32)
    diff = i[:, None] - i[None, :]
    dmask = jnp.where(diff >= 0, jnp.exp(log_g[:, None, None] * jnp.maximum(diff, 0.0)), 0.0)
    q_dec = jnp.exp(log_g[:, None] * (i + 1.0)).T[None, :, :, None]
    k_dec = jnp.exp(log_g[:, None] * (C - 1.0 - i)).T[None, :, :, None]
    c_dec = jnp.exp(log_g * C)[None, :, None, None]

    def chunks(t):
        return jnp.moveaxis(t.reshape(B, n, C, H, t.shape[-1]), 1, 0)

    def step(state, xs):
        qc, kc, vc = xs
        scores = jnp.einsum('bihd,bjhd->bhij', qc, kc) * dmask[None]
        y = jnp.einsum('bhij,bjhe->bihe', scores, vc)
        y = y + jnp.einsum('bihd,bhde->bihe', qc, state) * q_dec
        state = state * c_dec + jnp.einsum('bjhd,bjhe->bhde', kc * k_dec, vc)
        return state, y

    state0 = jnp.zeros((B, H, dk, dv), jnp.float32)
    _, ys = lax.scan(step, state0, (chunks(q), chunks(k), chunks(v)))
    return jnp.moveaxis(ys, 0, 1).reshape(B, S, H, dv)


def stick_breaking(q, k, v):
    B, S, H, d = q.shape
    n = S // SB_BLOCK
    scale = SB_DH ** -0.5
    qh = jnp.transpose(q, (0, 2, 1, 3))
    kh = jnp.transpose(k, (0, 2, 1, 3))
    vh = jnp.transpose(v, (0, 2, 1, 3))
    qb = jnp.moveaxis(qh.reshape(B, H, n, SB_BLOCK, d), 2, 0)
    kpos = jnp.arange(S)

    def block(args):
        qi, bi = args
        z = jnp.einsum('bhtd,bhsd->bhts', qi, kh).astype(jnp.float32) * scale
        tpos = bi * SB_BLOCK + jnp.arange(SB_BLOCK)
        mask = kpos[None, :] < tpos[:, None]
        log_stay = jnp.where(mask, jax.nn.log_sigmoid(-z), 0.0)
        later = lax.cumsum(log_stay, axis=3, reverse=True) - log_stay
        w = jnp.where(mask, jnp.exp(jax.nn.log_sigmoid(z) + later), 0.0)
        return jnp.einsum('bhts,bhsd->bhtd', w.astype(vh.dtype), vh)

    out = lax.map(block, (qb, jnp.arange(n)))
    out = jnp.moveaxis(out, 0, 2).reshape(B, H, S, d)
    return jnp.transpose(out, (0, 2, 1, 3)).reshape(B, S, H * d)


def peer(x, w_q, sub_keys_1, sub_keys_2, u_tab, v_tab):
    B, S, D = x.shape
    T = PEER_CHUNK
    n = S // T
    xc = jnp.moveaxis(x.reshape(B, n, T, D), 1, 0)
    k1 = sub_keys_1.astype(jnp.float32)
    k2 = sub_keys_2.astype(jnp.float32)

    def block(xb):
        q = (xb @ w_q).reshape(B, T, PEER_HEADS, 2, PEER_DKEY // 2).astype(jnp.float32)
        s1 = jnp.einsum('bthd,nd->bthn', q[..., 0, :], k1)
        s2 = jnp.einsum('bthd,nd->bthn', q[..., 1, :], k2)
        v1, i1 = lax.top_k(s1, PEER_TOPK)
        v2, i2 = lax.top_k(s2, PEER_TOPK)
        cand_s = (v1[..., :, None] + v2[..., None, :]).reshape(B, T, PEER_HEADS, PEER_TOPK * PEER_TOPK)
        cand_i = (i1[..., :, None] * N_KEYS + i2[..., None, :]).reshape(B, T, PEER_HEADS, PEER_TOPK * PEER_TOPK)
        top_s, pos = lax.top_k(cand_s, PEER_TOPK)
        idx = jnp.take_along_axis(cand_i, pos, axis=-1)
        g = jax.nn.softmax(top_s, axis=-1)
        u_sel = jnp.take(u_tab, idx, axis=0)
        h = jnp.einsum('bthkd,btd->bthk', u_sel, xb)
        a = (jax.nn.gelu(h.astype(jnp.float32)) * g).astype(xb.dtype)
        return jnp.einsum('bthk,bthkd->btd', a, jnp.take(v_tab, idx, axis=0))

    y = lax.map(block, xc)
    return jnp.moveaxis(y, 0, 1).reshape(B, S, D)


def setup_inputs(seed: int = 0) -> dict:
    key = jax.random.key(seed)
    ks = jax.random.split(key, 17)
    f = jnp.float32
    nrm = lambda k, shape, s: jax.random.normal(k, shape, f) * s
    gain = lambda k, shape: 1.0 + 0.02 * jax.random.normal(k, shape, f)
    return {
        'x': jax.random.normal(ks[0], (BATCH, SEQ, D_MODEL), f),
        'norm_attn': gain(ks[1], (DEPTH, D_MODEL)),
        'w_in': nrm(ks[2], (DEPTH, D_MODEL, IN_WIDTH), D_MODEL ** -0.5),
        'ret_q_norm': gain(ks[3], (DEPTH, RET_DK)),
        'ret_k_norm': gain(ks[4], (DEPTH, RET_DK)),
        'ret_group_norm': gain(ks[5], (DEPTH, RET_V)),
        'sb_q_norm': gain(ks[6], (DEPTH, SB_DH)),
        'sb_k_norm': gain(ks[7], (DEPTH, SB_DH)),
        'w_branch_ret': nrm(ks[8], (DEPTH, RET_V, D_MODEL), RET_V ** -0.5),
        'w_branch_sb': nrm(ks[9], (DEPTH, SB_W, D_MODEL), SB_W ** -0.5),
        'w_out': nrm(ks[10], (DEPTH, D_MODEL, D_MODEL), D_MODEL ** -0.5),
        'norm_ffn': gain(ks[11], (DEPTH, D_MODEL)),
        'peer_w_q': nrm(ks[12], (DEPTH, D_MODEL, PEER_HEADS * PEER_DKEY), D_MODEL ** -0.5),
        'peer_sub_keys_1': nrm(ks[13], (DEPTH, N_KEYS, PEER_DKEY // 2), (PEER_DKEY // 2) ** -0.5),
        'peer_sub_keys_2': nrm(ks[14], (DEPTH, N_KEYS, PEER_DKEY // 2), (PEER_DKEY // 2) ** -0.5),
        'peer_u': nrm(ks[15], (DEPTH, N_EXPERTS, D_MODEL), D_MODEL ** -0.5),
        'peer_v': nrm(ks[16], (DEPTH, N_EXPERTS, D_MODEL), PEER_HEADS ** -0.5),
    }


def reference(x, norm_attn, w_in, ret_q_norm, ret_k_norm, ret_group_norm, sb_q_norm, sb_k_norm,
              w_branch_ret, w_branch_sb, w_out, norm_ffn, peer_w_q, peer_sub_keys_1,
              peer_sub_keys_2, peer_u, peer_v):
    B, S, D = x.shape
    pos = jnp.arange(S, dtype=jnp.float32)
    split_points = np.cumsum(IN_SIZES)[:-1].tolist()
    for l in range(DEPTH):
        xn = rmsnorm(x, norm_attn[l])
        proj = xn @ w_in[l]
        rq, rk, rv, rg, sq, sk, sv, ga, gb = jnp.split(proj, split_points, axis=-1)

        rq = rope(rmsnorm(rq.reshape(B, S, RET_HEADS, RET_DK), ret_q_norm[l]), pos)
        rk = rope(rmsnorm(rk.reshape(B, S, RET_HEADS, RET_DK), ret_k_norm[l]), pos) * (RET_DK ** -0.5)
        ret = retention(rq, rk, rv.reshape(B, S, RET_HEADS, RET_DV).astype(jnp.float32))
        ret = group_norm(ret, ret_group_norm[l].reshape(RET_HEADS, RET_DV)).reshape(B, S, RET_V)
        ret = ret.astype(x.dtype) * jax.nn.silu(rg)
        y_a = ret @ w_branch_ret[l]

        sq = rmsnorm(sq.reshape(B, S, SB_HEADS, SB_DH), sb_q_norm[l])
        sk = rmsnorm(sk.reshape(B, S, SB_HEADS, SB_DH), sb_k_norm[l])
        sb = stick_breaking(sq, sk, sv.reshape(B, S, SB_HEADS, SB_DH))
        y_b = sb @ w_branch_sb[l]

        mixed = jax.nn.sigmoid(ga) * y_a + jax.nn.sigmoid(gb) * y_b
        x = x + mixed @ w_out[l]

        hn = rmsnorm(x, norm_ffn[l])
        x = x + peer(hn, peer_w_q[l], peer_sub_keys_1[l], peer_sub_keys_2[l], peer_u[l], peer_v[l])
    return x
```

```python
import functools

import jax
import jax.numpy as jnp
import numpy as np
from jax import lax
from jax.experimental import pallas as pl
from jax.experimental.pallas import tpu as pltpu

F32 = jnp.float32
BF16 = jnp.bfloat16

EPS = 1e-6
ROPE_BASE = 10000.0

RET_HEADS = 8
RET_DK = 64
RET_DV = 128
RET_CHUNK = 128
SB_HEADS = 8
SB_DH = 64
PEER_HEADS = 8
N_KEYS = 128
PEER_TOPK = 16
PEER_DKEY = 256

LANES = 128
SUBLANES = 8
HEADS_PER_BLOCK = 2
VMEM_LIMIT = 56 * 1024 * 1024

IN_TILE = 512
RET_TILE = 1024
SB_TILE = 256
SEL_TILE = 512
FFN_TOKENS = 512
FFN_EXPERTS = 1024

NEG_INF = float("-inf")


def _dot(a, b):
    return jnp.dot(a, b, preferred_element_type=F32)


def _dot_nt(a, b):
    return lax.dot_general(a, b, (((1,), (1,)), ((), ())), preferred_element_type=F32)


def _split_dot(a, b):
    hi = a.astype(BF16)
    lo = (a - hi.astype(F32)).astype(BF16)
    return _dot(hi, b) + _dot(lo, b)


def _sigmoid(x):
    return 1.0 / (1.0 + jnp.exp(-x))


def _in_proj_kernel(x_ref, g_ref, w_ref, cos_ref, sin_ref, hg_ref, gmat_ref,
                    rq_ref, rk_ref, rv_ref, rg_ref, sq_ref, sk_ref, sv_ref, ga_ref, gb_ref):
    xf = x_ref[...]
    ms = jnp.mean(xf * xf, axis=-1, keepdims=True)
    xn = (xf * lax.rsqrt(ms + EPS) * g_ref[...]).astype(BF16)
    gmat = gmat_ref[...]
    cos = cos_ref[...]
    sin = sin_ref[...]
    lane = lax.broadcasted_iota(jnp.int32, cos.shape, 1)
    first_half = (lane % RET_DK) < (RET_DK // 2)

    def head_norm(t, gain):
        return t * lax.rsqrt(_split_dot(t * t, gmat) + EPS) * gain

    def rope(y):
        partner = jnp.where(first_half, pltpu.roll(y, LANES - RET_DK // 2, 1),
                            pltpu.roll(y, RET_DK // 2, 1))
        return y * cos + partner * sin

    col = 0

    def proj(width):
        nonlocal col
        out = _dot(xn, w_ref[:, col:col + width])
        col += width
        return out

    qk = RET_HEADS * RET_DK
    t = proj(qk)
    for c in range(qk // LANES):
        blk = t[:, c * LANES:(c + 1) * LANES]
        rq_ref[:, c * LANES:(c + 1) * LANES] = rope(head_norm(blk, hg_ref[0:1, :])).astype(BF16)
    t = proj(qk)
    for c in range(qk // LANES):
        blk = t[:, c * LANES:(c + 1) * LANES]
        rk_ref[:, c * LANES:(c + 1) * LANES] = (
            rope(head_norm(blk, hg_ref[1:2, :])) * (RET_DK ** -0.5)).astype(BF16)
    rv_ref[...] = proj(RET_HEADS * RET_DV).astype(BF16)
    t = proj(RET_HEADS * RET_DV)
    rg_ref[...] = (t * _sigmoid(t)).astype(BF16)
    sw = SB_HEADS * SB_DH
    t = proj(sw)
    for c in range(sw // LANES):
        blk = t[:, c * LANES:(c + 1) * LANES]
        sq_ref[:, c * LANES:(c + 1) * LANES] = (
            head_norm(blk, hg_ref[2:3, :]) * (SB_DH ** -0.5)).astype(BF16)
    t = proj(sw)
    for c in range(sw // LANES):
        blk = t[:, c * LANES:(c + 1) * LANES]
        sk_ref[:, c * LANES:(c + 1) * LANES] = head_norm(blk, hg_ref[3:4, :]).astype(BF16)
    sv_ref[...] = proj(sw).astype(BF16)
    d_model = ga_ref.shape[1]
    ga_ref[...] = _sigmoid(proj(d_model)).astype(BF16)
    gb_ref[...] = _sigmoid(proj(d_model)).astype(BF16)


def _in_proj(x2d, norm_g, w_in, cos_t, sin_t, head_gains, gmat):
    s, d = x2d.shape
    width = w_in.shape[1]
    qk = RET_HEADS * RET_DK
    rv = RET_HEADS * RET_DV
    sw = SB_HEADS * SB_DH
    tm = IN_TILE
    row = lambda w: pl.BlockSpec((tm, w), lambda i: (i, 0))
    full = lambda a: pl.BlockSpec(a.shape, lambda i: (0,) * a.ndim)
    out_widths = (qk, qk, rv, rv, sw, sw, sw, d, d)
    return pl.pallas_call(
        _in_proj_kernel,
        grid=(s // tm,),
        in_specs=[row(d), full(norm_g), full(w_in), row(LANES), row(LANES), full(head_gains),
                  full(gmat)],
        out_specs=[row(w) for w in out_widths],
        out_shape=[jax.ShapeDtypeStruct((s, w), BF16) for w in out_widths],
        compiler_params=pltpu.CompilerParams(dimension_semantics=("arbitrary",),
                                             vmem_limit_bytes=VMEM_LIMIT),
        name="in_proj",
    )(x2d, norm_g, w_in, cos_t, sin_t, head_gains, gmat)


def _retention_kernel(q_ref, k_ref, v_ref, gate_ref, gn_ref, dmask_ref, qdec_ref, kdec_ref,
                      cdec_ref, bd_ref, o_ref, state_ref):
    @pl.when(pl.program_id(1) == 0)
    def _():
        state_ref[...] = jnp.zeros_like(state_ref)

    c = RET_CHUNK
    lane = lax.broadcasted_iota(jnp.int32, (c, LANES), 1)
    head0 = lane < RET_DK
    bd = bd_ref[...]
    qdec = qdec_ref[...]
    kdec = kdec_ref[...]
    cdec = cdec_ref[...]
    gn = gn_ref[...]

    def chunk(ci, carry):
        r0 = pl.multiple_of(ci * c, c)
        q = q_ref[pl.ds(r0, c), :]
        k = k_ref[pl.ds(r0, c), :]
        v = v_ref[pl.ds(r0, c), :]
        state = state_ref[...]
        zero = jnp.zeros_like(q)
        q_heads = (jnp.where(head0, q, zero), jnp.where(head0, zero, q))
        cross = _dot(q, state.astype(BF16)) * qdec
        outs = []
        for h in range(HEADS_PER_BLOCK):
            scores = _dot_nt(q_heads[h], k) * dmask_ref[h]
            y = _dot(scores.astype(BF16), v[:, h * RET_DV:(h + 1) * RET_DV])
            y = y + cross[:, h * RET_DV:(h + 1) * RET_DV]
            mu = jnp.mean(y, axis=-1, keepdims=True)
            yc = y - mu
            var = jnp.mean(yc * yc, axis=-1, keepdims=True)
            outs.append(yc * lax.rsqrt(var + EPS) * gn[:, h * RET_DV:(h + 1) * RET_DV])
        out = jnp.concatenate(outs, axis=1) * gate_ref[pl.ds(r0, c), :].astype(F32)
        o_ref[pl.ds(r0, c), :] = out.astype(o_ref.dtype)
        kd_t = (k.astype(F32) * kdec).T.astype(BF16)
        state_ref[...] = state * cdec + _dot(kd_t, v) * bd
        return carry

    lax.fori_loop(0, q_ref.shape[0] // c, chunk, 0)


def _retention(rq, rk, rv, gate, gn_gain, consts):
    s = rq.shape[0]
    dmask, qdec, kdec, cdec, bd = consts
    t = RET_TILE
    pairs = RET_HEADS // HEADS_PER_BLOCK
    vw = HEADS_PER_BLOCK * RET_DV
    return pl.pallas_call(
        _retention_kernel,
        grid=(pairs, s // t),
        in_specs=[
            pl.BlockSpec((t, LANES), lambda p, i: (i, p)),
            pl.BlockSpec((t, LANES), lambda p, i: (i, p)),
            pl.BlockSpec((t, vw), lambda p, i: (i, p)),
            pl.BlockSpec((t, vw), lambda p, i: (i, p)),
            pl.BlockSpec((1, vw), lambda p, i: (0, p)),
            pl.BlockSpec((HEADS_PER_BLOCK, RET_CHUNK, RET_CHUNK), lambda p, i: (p, 0, 0)),
            pl.BlockSpec((RET_CHUNK, vw), lambda p, i: (0, p)),
            pl.BlockSpec((RET_CHUNK, LANES), lambda p, i: (0, p)),
            pl.BlockSpec((1, vw), lambda p, i: (0, p)),
            pl.BlockSpec(bd.shape, lambda p, i: (0, 0)),
        ],
        out_specs=pl.BlockSpec((t, vw), lambda p, i: (i, p)),
        out_shape=jax.ShapeDtypeStruct((s, RET_HEADS * RET_DV), BF16),
        scratch_shapes=[pltpu.VMEM((LANES, vw), F32)],
        compiler_params=pltpu.CompilerParams(dimension_semantics=("arbitrary", "arbitrary"),
                                             vmem_limit_bytes=VMEM_LIMIT),
        name="retention",
    )(rq, rk, rv, gate, gn_gain, dmask, qdec, kdec, cdec, bd)


def _retention_consts():
    c = RET_CHUNK
    h = jnp.arange(RET_HEADS, dtype=F32)
    log_g = jnp.log1p(-jnp.exp2(-5.0 - h))
    i = jnp.arange(c, dtype=F32)
    diff = i[:, None] - i[None, :]
    dmask = jnp.where(diff >= 0, jnp.exp(log_g[:, None, None] * jnp.maximum(diff, 0.0)), 0.0)
    q_dec = jnp.exp(log_g[None, :] * (i[:, None] + 1.0))
    k_dec = jnp.exp(log_g[None, :] * (c - 1.0 - i[:, None]))
    c_dec = jnp.exp(log_g * c)
    qdec = jnp.repeat(q_dec, RET_DV, axis=1)
    kdec = jnp.repeat(k_dec, RET_DK, axis=1)
    cdec = jnp.repeat(c_dec, RET_DV)[None, :]
    rows = np.arange(LANES)[:, None] // RET_DK
    cols = np.arange(HEADS_PER_BLOCK * RET_DV)[None, :] // RET_DV
    bd = jnp.asarray((rows == cols).astype(np.float32))
    return dmask, qdec, kdec, cdec, bd


def _stickbreak_kernel(q_ref, k_ref, v_ref, u_ref, o_ref):
    t = SB_TILE
    i = pl.program_id(1)
    q = q_ref[...]
    lane = lax.broadcasted_iota(jnp.int32, (t, LANES), 1)
    head0 = lane < SB_DH
    zero = jnp.zeros_like(q)
    q_heads = (jnp.where(head0, q, zero), jnp.where(head0, zero, q))
    u = u_ref[...]
    row = lax.broadcasted_iota(jnp.int32, (t, t), 0)
    colm = lax.broadcasted_iota(jnp.int32, (t, t), 1)
    below = colm < row

    def tile(j, carry, diagonal):
        c0, c1, acc = carry
        cs = [c0, c1]
        r0 = pl.multiple_of(j * t, t)
        kt = k_ref[pl.ds(r0, t), :]
        vt = v_ref[pl.ds(r0, t), :]
        vzero = jnp.zeros_like(vt)
        v_heads = (jnp.where(head0, vt, vzero), jnp.where(head0, vzero, vt))
        for h in range(HEADS_PER_BLOCK):
            z = _dot_nt(q_heads[h], kt)
            sp = jnp.maximum(z, 0.0) + jnp.log(1.0 + jnp.exp(-jnp.abs(z)))
            log_stay = -sp
            log_beta = z - sp
            if diagonal:
                log_stay = jnp.where(below, log_stay, 0.0)
            later = _split_dot(log_stay, u) + cs[h]
            w = jnp.exp(log_beta + later)
            if diagonal:
                w = jnp.where(below, w, 0.0)
            acc = acc + _dot(w.astype(BF16), v_heads[h])
            cs[h] = cs[h] + jnp.sum(log_stay, axis=-1, keepdims=True)
        return cs[0], cs[1], acc

    init = (jnp.zeros((t, 1), F32), jnp.zeros((t, 1), F32), jnp.zeros((t, LANES), F32))
    carry = tile(i, init, True)
    carry = lax.fori_loop(0, i, lambda jj, c: tile(i - 1 - jj, c, False), carry)
    o_ref[...] = carry[2].astype(o_ref.dtype)


def _stickbreak(sq, sk, sv, umat):
    s = sq.shape[0]
    t = SB_TILE
    pairs = SB_HEADS // HEADS_PER_BLOCK
    return pl.pallas_call(
        _stickbreak_kernel,
        grid=(pairs, s // t),
        in_specs=[
            pl.BlockSpec((t, LANES), lambda p, i: (i, p)),
            pl.BlockSpec((s, LANES), lambda p, i: (0, p)),
            pl.BlockSpec((s, LANES), lambda p, i: (0, p)),
            pl.BlockSpec((t, t), lambda p, i: (0, 0)),
        ],
        out_specs=pl.BlockSpec((t, LANES), lambda p, i: (i, p)),
        out_shape=jax.ShapeDtypeStruct((s, SB_HEADS * SB_DH), BF16),
        compiler_params=pltpu.CompilerParams(dimension_semantics=("arbitrary", "arbitrary"),
                                             vmem_limit_bytes=VMEM_LIMIT),
        name="stickbrk",
    )(sq, sk, sv, umat)


def _out_proj_kernel(ret_ref, sb_ref, ga_ref, gb_ref, x_ref, wa_ref, wb_ref, wo_ref, g_ref,
                     wq_ref, k1_ref, k2_ref, x2_ref, hnt_ref, s1_ref, s2_ref):
    ya = _dot(ret_ref[...], wa_ref[...])
    yb = _dot(sb_ref[...], wb_ref[...])
    mixed = ga_ref[...].astype(F32) * ya + gb_ref[...].astype(F32) * yb
    x2 = x_ref[...] + _dot(mixed.astype(BF16), wo_ref[...])
    x2_ref[...] = x2
    ms = jnp.mean(x2 * x2, axis=-1, keepdims=True)
    hn = x2 * lax.rsqrt(ms + EPS) * g_ref[...]
    hnt_ref[...] = hn.T.astype(BF16)
    qp = _dot(hn.astype(BF16), wq_ref[...]).astype(BF16)
    k1 = k1_ref[...]
    k2 = k2_ref[...]
    half = PEER_DKEY // 2
    for h in range(PEER_HEADS):
        s1_ref[h] = _dot_nt(k1, qp[:, h * PEER_DKEY:h * PEER_DKEY + half])
        s2_ref[h] = _dot_nt(k2, qp[:, h * PEER_DKEY + half:(h + 1) * PEER_DKEY])


def _out_proj(ret, sb, ga, gb, x2d, wa, wb, wo, norm_g, wq, k1, k2):
    s, d = x2d.shape
    tm = IN_TILE
    row = lambda a: pl.BlockSpec((tm, a.shape[1]), lambda i: (i, 0))
    full = lambda a: pl.BlockSpec(a.shape, lambda i: (0,) * a.ndim)
    score_spec = pl.BlockSpec((PEER_HEADS, N_KEYS, tm), lambda i: (0, 0, i))
    score_shape = jax.ShapeDtypeStruct((PEER_HEADS, N_KEYS, s), F32)
    return pl.pallas_call(
        _out_proj_kernel,
        grid=(s // tm,),
        in_specs=[row(ret), row(sb), row(ga), row(gb), row(x2d), full(wa), full(wb), full(wo),
                  full(norm_g), full(wq), full(k1), full(k2)],
        out_specs=[pl.BlockSpec((tm, d), lambda i: (i, 0)), pl.BlockSpec((d, tm), lambda i: (0, i)),
                   score_spec, score_spec],
        out_shape=[jax.ShapeDtypeStruct((s, d), F32), jax.ShapeDtypeStruct((d, s), BF16),
                   score_shape, score_shape],
        compiler_params=pltpu.CompilerParams(dimension_semantics=("arbitrary",),
                                             vmem_limit_bytes=VMEM_LIMIT),
        name="out_proj",
    )(ret, sb, ga, gb, x2d, wa, wb, wo, norm_g, wq, k1, k2)


_CAND_GROUPS = [(a, PEER_TOPK // (a + 1)) for a in range(PEER_TOPK // 2)]


def _extract_top(x, count, row_index):
    rank = jnp.full(x.shape, float(PEER_TOPK), F32)
    vals = []
    big = float(x.shape[0])
    for r in range(count):
        m = jnp.max(x, axis=0, keepdims=True)
        first = jnp.min(jnp.where(x == m, row_index, big), axis=0, keepdims=True)
        hit = row_index == first
        rank = jnp.where(hit, float(r), rank)
        x = jnp.where(hit, NEG_INF, x)
        vals.append(m)
    return vals, rank


def _stack_rows(rows, row_index):
    out = jnp.zeros(row_index.shape, F32)
    for r, v in enumerate(rows):
        out = jnp.where(row_index == float(r), v, out)
    return out


def _peer_sel_kernel(s1_ref, s2_ref, rank2_ref, cnt_ref, e1_ref, e2_ref):
    k = PEER_TOPK
    key_index = lax.broadcasted_iota(jnp.int32, (N_KEYS, LANES), 0).astype(F32)
    idx16 = lax.broadcasted_iota(jnp.int32, (k, LANES), 0).astype(F32)
    idx8 = idx16[:8]
    n_cand_rows = 8 * (len(_CAND_GROUPS) + 2)
    cand_index = lax.broadcasted_iota(jnp.int32, (n_cand_rows, LANES), 0).astype(F32)
    chunks = s1_ref.shape[2] // LANES

    def unit(uidx, carry):
        h = uidx // chunks
        l0 = pl.multiple_of((uidx % chunks) * LANES, LANES)
        x1 = s1_ref[h, :, pl.ds(l0, LANES)]
        x2 = s2_ref[h, :, pl.ds(l0, LANES)]
        v1, rank1 = _extract_top(x1, k, key_index)
        v2, rank2 = _extract_top(x2, k, key_index)
        v1s = _stack_rows(v1, idx16)
        v2s = _stack_rows(v2, idx16)
        groups = [v1[0] + v2s]
        for a, n in _CAND_GROUPS[1:]:
            groups.append(jnp.where(idx8 < float(n), v1[a] + v2s[:8], NEG_INF))
        groups.append(v1s[8:] + v2[0])
        cand = jnp.concatenate(groups, axis=0)
        _, crank = _extract_top(cand, k, cand_index)
        sel = crank < float(k)
        top = v1[0] + v2[0]
        z = jnp.sum(jnp.where(sel, jnp.exp(cand - top), 0.0), axis=0, keepdims=True)
        self32 = sel.astype(F32)
        counts = [jnp.sum(self32[0:16], axis=0, keepdims=True)]
        for g in range(1, len(_CAND_GROUPS)):
            counts.append(jnp.sum(self32[8 + 8 * g:16 + 8 * g], axis=0, keepdims=True))
        tail = 8 + 8 * len(_CAND_GROUPS)
        for r in range(8):
            counts.append(self32[tail + r:tail + r + 1])
        cnt = jnp.zeros((N_KEYS, LANES), F32)
        for a in range(k):
            cnt = jnp.where(rank1 == float(a), counts[a], cnt)
        rank2_ref[h, :, pl.ds(l0, LANES)] = rank2
        cnt_ref[h, :, pl.ds(l0, LANES)] = cnt
        e1_ref[h, :, pl.ds(l0, LANES)] = jnp.exp(x1 - v1[0])
        e2_ref[h, :, pl.ds(l0, LANES)] = jnp.exp(x2 - v2[0]) * (1.0 / z)
        return carry

    lax.fori_loop(0, PEER_HEADS * chunks, unit, 0)


def _peer_sel(s1t, s2t):
    s = s1t.shape[2]
    t = SEL_TILE
    spec = pl.BlockSpec((PEER_HEADS, N_KEYS, t), lambda i: (0, 0, i))
    shape = jax.ShapeDtypeStruct(s1t.shape, F32)
    return pl.pallas_call(
        _peer_sel_kernel,
        grid=(s // t,),
        in_specs=[spec, spec],
        out_specs=[spec] * 4,
        out_shape=[shape] * 4,
        compiler_params=pltpu.CompilerParams(dimension_semantics=("arbitrary",),
                                             vmem_limit_bytes=VMEM_LIMIT),
        name="peer_sel",
    )(s1t, s2t)


def _gelu_tanh(x):
    return 0.5 * x * (1.0 + jnp.tanh(0.7978845608028654 * (x + 0.044715 * (x * x * x))))


def _peer_ffn_kernel(hnt_ref, u_ref, vt_ref, rank2_ref, cnt_ref, e1_ref, e2_ref, x2_ref, o_ref,
                     yt_ref, at_ref):
    e = pl.program_id(1)

    @pl.when(e == 0)
    def _():
        yt_ref[...] = jnp.zeros_like(yt_ref)

    rows = u_ref.shape[0] // N_KEYS
    assert rows == SUBLANES
    chunks = hnt_ref.shape[1] // LANES
    ht = _dot(u_ref[...], hnt_ref[...])
    i0 = pl.multiple_of(e * rows, rows)
    for r in range(rows):
        for c in range(chunks):
            ls = slice(c * LANES, (c + 1) * LANES)
            gate = jnp.zeros((N_KEYS, LANES), F32)
            for h in range(PEER_HEADS):
                n_row = cnt_ref[h, pl.ds(i0, rows), ls][r:r + 1]
                e1_row = e1_ref[h, pl.ds(i0, rows), ls][r:r + 1]
                gate = gate + jnp.where(rank2_ref[h, :, ls] < n_row, e2_ref[h, :, ls], 0.0) * e1_row
            hblk = ht[r * N_KEYS:(r + 1) * N_KEYS, ls]
            at_ref[r * N_KEYS:(r + 1) * N_KEYS, ls] = (_gelu_tanh(hblk) * gate).astype(BF16)
    yt_ref[...] += _dot(vt_ref[...], at_ref[...])

    @pl.when(e == pl.num_programs(1) - 1)
    def _():
        o_ref[...] = x2_ref[...] + yt_ref[...].T


def _peer_ffn(hnt, u_bf, vt_bf, rank2, cnt, e1, e2, x2):
    s, d = x2.shape
    n_exp = u_bf.shape[0]
    tt = FFN_TOKENS
    te = FFN_EXPERTS
    sel_spec = pl.BlockSpec((PEER_HEADS, N_KEYS, tt), lambda t, e: (0, 0, t))
    return pl.pallas_call(
        _peer_ffn_kernel,
        grid=(s // tt, n_exp // te),
        in_specs=[
            pl.BlockSpec((d, tt), lambda t, e: (0, t)),
            pl.BlockSpec((te, d), lambda t, e: (e, 0)),
            pl.BlockSpec((d, te), lambda t, e: (0, e)),
            sel_spec, sel_spec, sel_spec, sel_spec,
            pl.BlockSpec((tt, d), lambda t, e: (t, 0)),
        ],
        out_specs=pl.BlockSpec((tt, d), lambda t, e: (t, 0)),
        out_shape=jax.ShapeDtypeStruct((s, d), F32),
        scratch_shapes=[pltpu.VMEM((d, tt), F32), pltpu.VMEM((te, tt), BF16)],
        compiler_params=pltpu.CompilerParams(dimension_semantics=("arbitrary", "arbitrary"),
                                             vmem_limit_bytes=VMEM_LIMIT),
        name="peer_ffn",
    )(hnt, u_bf, vt_bf, rank2, cnt, e1, e2, x2)


def _rope_tables(s):
    half = RET_DK // 2
    pos = jnp.arange(s, dtype=F32)
    freqs = ROPE_BASE ** (-jnp.arange(half, dtype=F32) / half)
    ang = pos[:, None] * freqs[None, :]
    cos = jnp.cos(ang)
    sin = jnp.sin(ang)
    reps = LANES // RET_DK
    cos_t = jnp.tile(jnp.concatenate([cos, cos], axis=-1), (1, reps))
    sin_t = jnp.tile(jnp.concatenate([-sin, sin], axis=-1), (1, reps))
    return cos_t, sin_t


def kernel(x, norm_attn, w_in, ret_q_norm, ret_k_norm, ret_group_norm, sb_q_norm, sb_k_norm,
           w_branch_ret, w_branch_sb, w_out, norm_ffn, peer_w_q, peer_sub_keys_1,
           peer_sub_keys_2, peer_u, peer_v):
    b, s, d = x.shape
    assert b == 1 and s % RET_TILE == 0 and s % IN_TILE == 0 and s % FFN_TOKENS == 0
    depth = w_in.shape[0]
    cos_t, sin_t = _rope_tables(s)
    ret_consts = _retention_consts()
    grp = np.arange(LANES) // RET_DK
    gmat = jnp.asarray((grp[:, None] == grp[None, :]).astype(np.float32) / RET_DK, dtype=BF16)
    kk = np.arange(SB_TILE)
    umat = jnp.asarray((kk[:, None] > kk[None, :]).astype(np.float32), dtype=BF16)
    reps = LANES // RET_DK
    xc = x.reshape(s, d)
    for l in range(depth):
        head_gains = jnp.stack([jnp.tile(g[l], reps) for g in
                                (ret_q_norm, ret_k_norm, sb_q_norm, sb_k_norm)])
        rq, rk, rv, rg, sq, sk, sv, ga, gb = _in_proj(
            xc, norm_attn[l][None, :], w_in[l].astype(BF16), cos_t, sin_t, head_gains, gmat)
        ret = _retention(rq, rk, rv, rg, ret_group_norm[l][None, :], ret_consts)
        sb = _stickbreak(sq, sk, sv, umat)
        x2, hnt, s1t, s2t = _out_proj(
            ret, sb, ga, gb, xc, w_branch_ret[l].astype(BF16), w_branch_sb[l].astype(BF16),
            w_out[l].astype(BF16), norm_ffn[l][None, :], peer_w_q[l].astype(BF16),
            peer_sub_keys_1[l].astype(BF16), peer_sub_keys_2[l].astype(BF16))
        rank2, cnt, e1, e2 = _peer_sel(s1t, s2t)
        xc = _peer_ffn(hnt, peer_u[l].astype(BF16), peer_v[l].T.astype(BF16), rank2, cnt, e1, e2,
                       x2)
    return xc.reshape(b, s, d)
```

```python
import functools

import jax
import jax.numpy as jnp
import numpy as np
from jax import lax
from jax.experimental import pallas as pl
from jax.experimental.pallas import tpu as pltpu

F32 = jnp.float32
BF16 = jnp.bfloat16

EPS = 1e-6
ROPE_BASE = 10000.0

RET_HEADS = 8
RET_DK = 64
RET_DV = 128
RET_CHUNK = 128
SB_HEADS = 8
SB_DH = 64
PEER_HEADS = 8
N_KEYS = 128
PEER_TOPK = 16
PEER_DKEY = 256

LANES = 128
SUBLANES = 8
HEADS_PER_BLOCK = 2
VMEM_LIMIT = 56 * 1024 * 1024

IN_TILE = 512
RET_TILE = 1024
SB_TILE = 256
SEL_TILE = 512
FFN_TOKENS = 512
FFN_EXPERTS = 1024

NEG_INF = float("-inf")
MASKED_LOG = -1e30


def _dot(a, b):
    return jnp.dot(a, b, preferred_element_type=F32)


def _dot_nt(a, b):
    return lax.dot_general(a, b, (((1,), (1,)), ((), ())), preferred_element_type=F32)


def _split_dot(a, b):
    hi = a.astype(BF16)
    lo = (a - hi.astype(F32)).astype(BF16)
    return _dot(hi, b) + _dot(lo, b)


def _sigmoid(x):
    return 1.0 / (1.0 + jnp.exp(-x))


def _in_proj_kernel(x_ref, g_ref, w_ref, cos_ref, sin_ref, hg_ref, gmat_ref,
                    rq_ref, rk_ref, rv_ref, rg_ref, sq_ref, sk_ref, sv_ref, ga_ref, gb_ref):
    xf = x_ref[...]
    ms = jnp.mean(xf * xf, axis=-1, keepdims=True)
    xn = (xf * lax.rsqrt(ms + EPS) * g_ref[...]).astype(BF16)
    gmat = gmat_ref[...]
    cos = cos_ref[...]
    sin = sin_ref[...]
    lane = lax.broadcasted_iota(jnp.int32, cos.shape, 1)
    first_half = (lane % RET_DK) < (RET_DK // 2)

    def head_norm(t, gain):
        return t * lax.rsqrt(_split_dot(t * t, gmat) + EPS) * gain

    def rope(y):
        partner = jnp.where(first_half, pltpu.roll(y, LANES - RET_DK // 2, 1),
                            pltpu.roll(y, RET_DK // 2, 1))
        return y * cos + partner * sin

    col = 0

    def proj(width):
        nonlocal col
        out = _dot(xn, w_ref[:, col:col + width])
        col += width
        return out

    qk = RET_HEADS * RET_DK
    t = proj(qk)
    for c in range(qk // LANES):
        blk = t[:, c * LANES:(c + 1) * LANES]
        rq_ref[:, c * LANES:(c + 1) * LANES] = rope(head_norm(blk, hg_ref[0:1, :])).astype(BF16)
    t = proj(qk)
    for c in range(qk // LANES):
        blk = t[:, c * LANES:(c + 1) * LANES]
        rk_ref[:, c * LANES:(c + 1) * LANES] = (
            rope(head_norm(blk, hg_ref[1:2, :])) * (RET_DK ** -0.5)).astype(BF16)
    rv_ref[...] = proj(RET_HEADS * RET_DV).astype(BF16)
    t = proj(RET_HEADS * RET_DV)
    rg_ref[...] = (t * _sigmoid(t)).astype(BF16)
    sw = SB_HEADS * SB_DH
    t = proj(sw)
    for c in range(sw // LANES):
        blk = t[:, c * LANES:(c + 1) * LANES]
        sq_ref[:, c * LANES:(c + 1) * LANES] = (
            head_norm(blk, hg_ref[2:3, :]) * (SB_DH ** -0.5)).astype(BF16)
    t = proj(sw)
    for c in range(sw // LANES):
        blk = t[:, c * LANES:(c + 1) * LANES]
        sk_ref[:, c * LANES:(c + 1) * LANES] = head_norm(blk, hg_ref[3:4, :]).astype(BF16)
    sv_ref[...] = proj(sw).astype(BF16)
    d_model = ga_ref.shape[1]
    ga_ref[...] = _sigmoid(proj(d_model)).astype(BF16)
    gb_ref[...] = _sigmoid(proj(d_model)).astype(BF16)


def _in_proj(x2d, norm_g, w_in, cos_t, sin_t, head_gains, gmat):
    s, d = x2d.shape
    width = w_in.shape[1]
    qk = RET_HEADS * RET_DK
    rv = RET_HEADS * RET_DV
    sw = SB_HEADS * SB_DH
    tm = IN_TILE
    row = lambda w: pl.BlockSpec((tm, w), lambda i: (i, 0))
    full = lambda a: pl.BlockSpec(a.shape, lambda i: (0,) * a.ndim)
    out_widths = (qk, qk, rv, rv, sw, sw, sw, d, d)
    return pl.pallas_call(
        _in_proj_kernel,
        grid=(s // tm,),
        in_specs=[row(d), full(norm_g), full(w_in), row(LANES), row(LANES), full(head_gains),
                  full(gmat)],
        out_specs=[row(w) for w in out_widths],
        out_shape=[jax.ShapeDtypeStruct((s, w), BF16) for w in out_widths],
        compiler_params=pltpu.CompilerParams(dimension_semantics=("arbitrary",),
                                             vmem_limit_bytes=VMEM_LIMIT),
        name="in_proj",
    )(x2d, norm_g, w_in, cos_t, sin_t, head_gains, gmat)


def _retention_kernel(q_ref, k_ref, v_ref, gate_ref, gn_ref, dmask_ref, qdec_ref, kdec_ref,
                      cdec_ref, bd_ref, o_ref, state_ref):
    @pl.when(pl.program_id(1) == 0)
    def _():
        state_ref[...] = jnp.zeros_like(state_ref)

    c = RET_CHUNK
    lane = lax.broadcasted_iota(jnp.int32, (c, LANES), 1)
    head0 = lane < RET_DK
    bd = bd_ref[...]
    qdec = qdec_ref[...]
    kdec = kdec_ref[...]
    cdec = cdec_ref[...]
    gn = gn_ref[...]

    def chunk(ci, carry):
        r0 = pl.multiple_of(ci * c, c)
        q = q_ref[pl.ds(r0, c), :]
        k = k_ref[pl.ds(r0, c), :]
        v = v_ref[pl.ds(r0, c), :]
        state = state_ref[...]
        zero = jnp.zeros_like(q)
        q_heads = (jnp.where(head0, q, zero), jnp.where(head0, zero, q))
        cross = _dot(q, state.astype(BF16)) * qdec
        outs = []
        for h in range(HEADS_PER_BLOCK):
            scores = _dot_nt(q_heads[h], k) * dmask_ref[h]
            y = _dot(scores.astype(BF16), v[:, h * RET_DV:(h + 1) * RET_DV])
            y = y + cross[:, h * RET_DV:(h + 1) * RET_DV]
            mu = jnp.mean(y, axis=-1, keepdims=True)
            yc = y - mu
            var = jnp.mean(yc * yc, axis=-1, keepdims=True)
            outs.append(yc * lax.rsqrt(var + EPS) * gn[:, h * RET_DV:(h + 1) * RET_DV])
        out = jnp.concatenate(outs, axis=1) * gate_ref[pl.ds(r0, c), :].astype(F32)
        o_ref[pl.ds(r0, c), :] = out.astype(o_ref.dtype)
        kd_t = (k.astype(F32) * kdec).T.astype(BF16)
        state_ref[...] = state * cdec + _dot(kd_t, v) * bd
        return carry

    lax.fori_loop(0, q_ref.shape[0] // c, chunk, 0)


def _retention(rq, rk, rv, gate, gn_gain, consts):
    s = rq.shape[0]
    dmask, qdec, kdec, cdec, bd = consts
    t = RET_TILE
    pairs = RET_HEADS // HEADS_PER_BLOCK
    vw = HEADS_PER_BLOCK * RET_DV
    return pl.pallas_call(
        _retention_kernel,
        grid=(pairs, s // t),
        in_specs=[
            pl.BlockSpec((t, LANES), lambda p, i: (i, p)),
            pl.BlockSpec((t, LANES), lambda p, i: (i, p)),
            pl.BlockSpec((t, vw), lambda p, i: (i, p)),
            pl.BlockSpec((t, vw), lambda p, i: (i, p)),
            pl.BlockSpec((1, vw), lambda p, i: (0, p)),
            pl.BlockSpec((HEADS_PER_BLOCK, RET_CHUNK, RET_CHUNK), lambda p, i: (p, 0, 0)),
            pl.BlockSpec((RET_CHUNK, vw), lambda p, i: (0, p)),
            pl.BlockSpec((RET_CHUNK, LANES), lambda p, i: (0, p)),
            pl.BlockSpec((1, vw), lambda p, i: (0, p)),
            pl.BlockSpec(bd.shape, lambda p, i: (0, 0)),
        ],
        out_specs=pl.BlockSpec((t, vw), lambda p, i: (i, p)),
        out_shape=jax.ShapeDtypeStruct((s, RET_HEADS * RET_DV), BF16),
        scratch_shapes=[pltpu.VMEM((LANES, vw), F32)],
        compiler_params=pltpu.CompilerParams(dimension_semantics=("arbitrary", "arbitrary"),
                                             vmem_limit_bytes=VMEM_LIMIT),
        name="retention",
    )(rq, rk, rv, gate, gn_gain, dmask, qdec, kdec, cdec, bd)


def _retention_consts():
    c = RET_CHUNK
    h = jnp.arange(RET_HEADS, dtype=F32)
    log_g = jnp.log1p(-jnp.exp2(-5.0 - h))
    i = jnp.arange(c, dtype=F32)
    diff = i[:, None] - i[None, :]
    dmask = jnp.where(diff >= 0, jnp.exp(log_g[:, None, None] * jnp.maximum(diff, 0.0)), 0.0)
    q_dec = jnp.exp(log_g[None, :] * (i[:, None] + 1.0))
    k_dec = jnp.exp(log_g[None, :] * (c - 1.0 - i[:, None]))
    c_dec = jnp.exp(log_g * c)
    qdec = jnp.repeat(q_dec, RET_DV, axis=1)
    kdec = jnp.repeat(k_dec, RET_DK, axis=1)
    cdec = jnp.repeat(c_dec, RET_DV)[None, :]
    rows = np.arange(LANES)[:, None] // RET_DK
    cols = np.arange(HEADS_PER_BLOCK * RET_DV)[None, :] // RET_DV
    bd = jnp.asarray((rows == cols).astype(np.float32))
    return dmask, qdec, kdec, cdec, bd


def _stickbreak_kernel(q_ref, k_ref, v_ref, negu_ref, o_ref, z_ref, hl_ref, rs_ref, w_ref):
    t = SB_TILE
    i = pl.program_id(1)
    n_tiles = i + 1
    q = q_ref[...]
    lane = lax.broadcasted_iota(jnp.int32, (t, LANES), 1)
    head0 = lane < SB_DH
    zero = jnp.zeros_like(q)
    q_both = jnp.concatenate([jnp.where(head0, q, zero), jnp.where(head0, zero, q)], axis=0)

    def key_rows(m):
        j = jnp.clip(i - m, 0, i)
        return pl.ds(pl.multiple_of(j * t, t), t)

    def slot3(m):
        return (m + 3) % 3

    def slot2(m):
        return (m + 2) % 2

    def scores(m):
        z_ref[slot3(m)] = _dot_nt(q_both, k_ref[key_rows(m), :])

    def softplus_terms(m, diagonal):
        z = z_ref[slot3(m)]
        sp = jnp.maximum(z, 0.0) + jnp.log(1.0 + jnp.exp(-jnp.abs(z)))
        if diagonal:
            row = lax.broadcasted_iota(jnp.int32, (2 * t, t), 0) % t
            col = lax.broadcasted_iota(jnp.int32, (2 * t, t), 1)
            sp = jnp.where(col < row, sp, 0.0)
            z_ref[slot3(m)] = jnp.where(col < row, z, MASKED_LOG)
        hi = sp.astype(BF16)
        hl_ref[slot2(m), :, pl.ds(0, t)] = hi
        hl_ref[slot2(m), :, pl.ds(t, t)] = (sp - hi.astype(F32)).astype(BF16)
        rs_ref[slot2(m)] = jnp.sum(sp, axis=-1, keepdims=True)

    def weights(m, c, live=None):
        loc = _dot(hl_ref[slot2(m)], negu_ref[...])
        c_eff = c if live is None else jnp.where(live, c, -MASKED_LOG)
        w_ref[slot2(m)] = jnp.exp(z_ref[slot3(m)] + loc - c_eff).astype(BF16)
        return c + rs_ref[slot2(m)]

    def weighted_values(m, acc):
        vt = v_ref[key_rows(m), :]
        vzero = jnp.zeros_like(vt)
        w = w_ref[slot2(m)]
        return (acc + _dot(w[:t], jnp.where(head0, vt, vzero))
                + _dot(w[t:], jnp.where(head0, vzero, vt)))

    w_ref[1] = jnp.zeros(w_ref.shape[1:], BF16)
    scores(0)
    scores(1)
    softplus_terms(0, True)
    n_run = jnp.maximum(n_tiles, 2)

    def body(n, carry):
        c, acc = carry
        acc = weighted_values(n - 3, acc)
        c = weights(n - 2, c)
        softplus_terms(n - 1, False)
        scores(n)
        return c, acc

    init = (jnp.zeros((2 * t, 1), F32), jnp.zeros((t, LANES), F32))
    c, acc = lax.fori_loop(2, n_run, body, init)
    acc = weighted_values(n_run - 3, acc)
    c = weights(n_run - 2, c)
    softplus_terms(n_run - 1, False)
    acc = weighted_values(n_run - 2, acc)
    weights(n_run - 1, c, live=n_tiles > 1)
    acc = weighted_values(n_run - 1, acc)
    o_ref[...] = acc.astype(o_ref.dtype)


def _stickbreak(sq, sk, sv, negu):
    s = sq.shape[0]
    t = SB_TILE
    pairs = SB_HEADS // HEADS_PER_BLOCK
    return pl.pallas_call(
        _stickbreak_kernel,
        grid=(pairs, s // t),
        in_specs=[
            pl.BlockSpec((t, LANES), lambda p, i: (i, p)),
            pl.BlockSpec((s, LANES), lambda p, i: (0, p)),
            pl.BlockSpec((s, LANES), lambda p, i: (0, p)),
            pl.BlockSpec((2 * t, t), lambda p, i: (0, 0)),
        ],
        out_specs=pl.BlockSpec((t, LANES), lambda p, i: (i, p)),
        out_shape=jax.ShapeDtypeStruct((s, SB_HEADS * SB_DH), BF16),
        scratch_shapes=[
            pltpu.VMEM((3, 2 * t, t), F32),
            pltpu.VMEM((2, 2 * t, 2 * t), BF16),
            pltpu.VMEM((2, 2 * t, 1), F32),
            pltpu.VMEM((2, 2 * t, t), BF16),
        ],
        compiler_params=pltpu.CompilerParams(dimension_semantics=("arbitrary", "arbitrary"),
                                             vmem_limit_bytes=VMEM_LIMIT),
        name="stickbrk",
    )(sq, sk, sv, negu)


def _out_proj_kernel(ret_ref, sb_ref, ga_ref, gb_ref, x_ref, wa_ref, wb_ref, wo_ref, g_ref,
                     wq_ref, k1_ref, k2_ref, x2_ref, hnt_ref, s1_ref, s2_ref):
    ya = _dot(ret_ref[...], wa_ref[...])
    yb = _dot(sb_ref[...], wb_ref[...])
    mixed = ga_ref[...].astype(F32) * ya + gb_ref[...].astype(F32) * yb
    x2 = x_ref[...] + _dot(mixed.astype(BF16), wo_ref[...])
    x2_ref[...] = x2
    ms = jnp.mean(x2 * x2, axis=-1, keepdims=True)
    hn = x2 * lax.rsqrt(ms + EPS) * g_ref[...]
    hnt_ref[...] = hn.T.astype(BF16)
    qp = _dot(hn.astype(BF16), wq_ref[...]).astype(BF16)
    k1 = k1_ref[...]
    k2 = k2_ref[...]
    half = PEER_DKEY // 2
    for h in range(PEER_HEADS):
        s1_ref[h] = _dot_nt(k1, qp[:, h * PEER_DKEY:h * PEER_DKEY + half])
        s2_ref[h] = _dot_nt(k2, qp[:, h * PEER_DKEY + half:(h + 1) * PEER_DKEY])


def _out_proj(ret, sb, ga, gb, x2d, wa, wb, wo, norm_g, wq, k1, k2):
    s, d = x2d.shape
    tm = IN_TILE
    row = lambda a: pl.BlockSpec((tm, a.shape[1]), lambda i: (i, 0))
    full = lambda a: pl.BlockSpec(a.shape, lambda i: (0,) * a.ndim)
    score_spec = pl.BlockSpec((PEER_HEADS, N_KEYS, tm), lambda i: (0, 0, i))
    score_shape = jax.ShapeDtypeStruct((PEER_HEADS, N_KEYS, s), F32)
    return pl.pallas_call(
        _out_proj_kernel,
        grid=(s // tm,),
        in_specs=[row(ret), row(sb), row(ga), row(gb), row(x2d), full(wa), full(wb), full(wo),
                  full(norm_g), full(wq), full(k1), full(k2)],
        out_specs=[pl.BlockSpec((tm, d), lambda i: (i, 0)), pl.BlockSpec((d, tm), lambda i: (0, i)),
                   score_spec, score_spec],
        out_shape=[jax.ShapeDtypeStruct((s, d), F32), jax.ShapeDtypeStruct((d, s), BF16),
                   score_shape, score_shape],
        compiler_params=pltpu.CompilerParams(dimension_semantics=("arbitrary",),
                                             vmem_limit_bytes=VMEM_LIMIT),
        name="out_proj",
    )(ret, sb, ga, gb, x2d, wa, wb, wo, norm_g, wq, k1, k2)


_CAND_GROUPS = [(a, PEER_TOPK // (a + 1)) for a in range(PEER_TOPK // 2)]


def _extract_top(x, count, row_index):
    rank = jnp.full(x.shape, float(PEER_TOPK), F32)
    vals = []
    big = float(x.shape[0])
    for r in range(count):
        m = jnp.max(x, axis=0, keepdims=True)
        first = jnp.min(jnp.where(x == m, row_index, big), axis=0, keepdims=True)
        hit = row_index == first
        rank = jnp.where(hit, float(r), rank)
        x = jnp.where(hit, NEG_INF, x)
        vals.append(m)
    return vals, rank


def _stack_rows(rows, row_index):
    out = jnp.zeros(row_index.shape, F32)
    for r, v in enumerate(rows):
        out = jnp.where(row_index == float(r), v, out)
    return out


def _peer_sel_kernel(s1_ref, s2_ref, rank2_ref, cnt_ref, e1_ref, e2_ref):
    k = PEER_TOPK
    key_index = lax.broadcasted_iota(jnp.int32, (N_KEYS, LANES), 0).astype(F32)
    idx16 = lax.broadcasted_iota(jnp.int32, (k, LANES), 0).astype(F32)
    idx8 = idx16[:8]
    n_cand_rows = 8 * (len(_CAND_GROUPS) + 2)
    cand_index = lax.broadcasted_iota(jnp.int32, (n_cand_rows, LANES), 0).astype(F32)
    chunks = s1_ref.shape[2] // LANES

    def unit(uidx, carry):
        h = uidx // chunks
        l0 = pl.multiple_of((uidx % chunks) * LANES, LANES)
        x1 = s1_ref[h, :, pl.ds(l0, LANES)]
        x2 = s2_ref[h, :, pl.ds(l0, LANES)]
        v1, rank1 = _extract_top(x1, k, key_index)
        v2, rank2 = _extract_top(x2, k, key_index)
        v1s = _stack_rows(v1, idx16)
        v2s = _stack_rows(v2, idx16)
        groups = [v1[0] + v2s]
        for a, n in _CAND_GROUPS[1:]:
            groups.append(jnp.where(idx8 < float(n), v1[a] + v2s[:8], NEG_INF))
        groups.append(v1s[8:] + v2[0])
        cand = jnp.concatenate(groups, axis=0)
        _, crank = _extract_top(cand, k, cand_index)
        sel = crank < float(k)
        top = v1[0] + v2[0]
        z = jnp.sum(jnp.where(sel, jnp.exp(cand - top), 0.0), axis=0, keepdims=True)
        self32 = sel.astype(F32)
        counts = [jnp.sum(self32[0:16], axis=0, keepdims=True)]
        for g in range(1, len(_CAND_GROUPS)):
            counts.append(jnp.sum(self32[8 + 8 * g:16 + 8 * g], axis=0, keepdims=True))
        tail = 8 + 8 * len(_CAND_GROUPS)
        for r in range(8):
            counts.append(self32[tail + r:tail + r + 1])
        cnt = jnp.zeros((N_KEYS, LANES), F32)
        for a in range(k):
            cnt = jnp.where(rank1 == float(a), counts[a], cnt)
        rank2_ref[h, :, pl.ds(l0, LANES)] = rank2.astype(BF16)
        cnt_ref[h, :, pl.ds(l0, LANES)] = cnt
        e1_ref[h, :, pl.ds(l0, LANES)] = jnp.exp(x1 - v1[0])
        e2_ref[h, :, pl.ds(l0, LANES)] = (jnp.exp(x2 - v2[0]) * (1.0 / z)).astype(BF16)
        return carry

    lax.fori_loop(0, PEER_HEADS * chunks, unit, 0)


def _peer_sel(s1t, s2t):
    s = s1t.shape[2]
    t = SEL_TILE
    spec = pl.BlockSpec((PEER_HEADS, N_KEYS, t), lambda i: (0, 0, i))
    shape = jax.ShapeDtypeStruct(s1t.shape, F32)
    return pl.pallas_call(
        _peer_sel_kernel,
        grid=(s // t,),
        in_specs=[spec, spec],
        out_specs=[spec] * 4,
        out_shape=[jax.ShapeDtypeStruct(s1t.shape, BF16), shape, shape,
                   jax.ShapeDtypeStruct(s1t.shape, BF16)],
        compiler_params=pltpu.CompilerParams(dimension_semantics=("arbitrary",),
                                             vmem_limit_bytes=VMEM_LIMIT),
        name="peer_sel",
    )(s1t, s2t)


def _gelu_tanh(x):
    return 0.5 * x * (1.0 + jnp.tanh(0.7978845608028654 * (x + 0.044715 * (x * x * x))))


def _peer_ffn_kernel(hnt_ref, u_ref, vt_ref, rank2_ref, cnt_ref, e1_ref, e2_ref, x2_ref, o_ref,
                     yt_ref, at_ref):
    e = pl.program_id(1)

    @pl.when(e == 0)
    def _():
        yt_ref[...] = jnp.zeros_like(yt_ref)

    rows = u_ref.shape[0] // N_KEYS
    assert rows == SUBLANES
    chunks = hnt_ref.shape[1] // LANES
    ht = _dot(u_ref[...], hnt_ref[...])
    i0 = pl.multiple_of(e * rows, rows)
    packed_rows = 2 * SUBLANES
    zero = jnp.zeros((N_KEYS, LANES), BF16)

    def row_bf16(ref, h, r, ls):
        row = ref[h, pl.ds(i0, rows), ls][r:r + 1]
        packed = jnp.broadcast_to(row, (packed_rows, LANES)).astype(BF16)
        return jnp.tile(packed, (N_KEYS // packed_rows, 1))

    for r in range(rows):
        for c in range(chunks):
            ls = slice(c * LANES, (c + 1) * LANES)
            gate = zero
            for h in range(PEER_HEADS):
                selected = rank2_ref[h, :, ls] < row_bf16(cnt_ref, h, r, ls)
                gate = gate + jnp.where(selected, e2_ref[h, :, ls], zero) * row_bf16(e1_ref, h, r, ls)
            hblk = ht[r * N_KEYS:(r + 1) * N_KEYS, ls].astype(BF16)
            at_ref[r * N_KEYS:(r + 1) * N_KEYS, ls] = _gelu_tanh(hblk) * gate
    yt_ref[...] += _dot(vt_ref[...], at_ref[...])

    @pl.when(e == pl.num_programs(1) - 1)
    def _():
        o_ref[...] = x2_ref[...] + yt_ref[...].T


def _peer_ffn(hnt, u_bf, vt_bf, rank2, cnt, e1, e2, x2):
    s, d = x2.shape
    n_exp = u_bf.shape[0]
    tt = FFN_TOKENS
    te = FFN_EXPERTS
    sel_spec = pl.BlockSpec((PEER_HEADS, N_KEYS, tt), lambda t, e: (0, 0, t))
    return pl.pallas_call(
        _peer_ffn_kernel,
        grid=(s // tt, n_exp // te),
        in_specs=[
            pl.BlockSpec((d, tt), lambda t, e: (0, t)),
            pl.BlockSpec((te, d), lambda t, e: (e, 0)),
            pl.BlockSpec((d, te), lambda t, e: (0, e)),
            sel_spec, sel_spec, sel_spec, sel_spec,
            pl.BlockSpec((tt, d), lambda t, e: (t, 0)),
        ],
        out_specs=pl.BlockSpec((tt, d), lambda t, e: (t, 0)),
        out_shape=jax.ShapeDtypeStruct((s, d), F32),
        scratch_shapes=[pltpu.VMEM((d, tt), F32), pltpu.VMEM((te, tt), BF16)],
        compiler_params=pltpu.CompilerParams(dimension_semantics=("arbitrary", "arbitrary"),
                                             vmem_limit_bytes=VMEM_LIMIT),
        name="peer_ffn",
    )(hnt, u_bf, vt_bf, rank2, cnt, e1, e2, x2)


def _rope_tables(s):
    half = RET_DK // 2
    pos = jnp.arange(s, dtype=F32)
    freqs = ROPE_BASE ** (-jnp.arange(half, dtype=F32) / half)
    ang = pos[:, None] * freqs[None, :]
    cos = jnp.cos(ang)
    sin = jnp.sin(ang)
    reps = LANES // RET_DK
    cos_t = jnp.tile(jnp.concatenate([cos, cos], axis=-1), (1, reps))
    sin_t = jnp.tile(jnp.concatenate([-sin, sin], axis=-1), (1, reps))
    return cos_t, sin_t


def kernel(x, norm_attn, w_in, ret_q_norm, ret_k_norm, ret_group_norm, sb_q_norm, sb_k_norm,
           w_branch_ret, w_branch_sb, w_out, norm_ffn, peer_w_q, peer_sub_keys_1,
           peer_sub_keys_2, peer_u, peer_v):
    b, s, d = x.shape
    assert b == 1 and s % RET_TILE == 0 and s % IN_TILE == 0 and s % FFN_TOKENS == 0
    depth = w_in.shape[0]
    cos_t, sin_t = _rope_tables(s)
    ret_consts = _retention_consts()
    grp = np.arange(LANES) // RET_DK
    gmat = jnp.asarray((grp[:, None] == grp[None, :]).astype(np.float32) / RET_DK, dtype=BF16)
    kk = np.arange(SB_TILE)
    tri = -(kk[:, None] >= kk[None, :]).astype(np.float32)
    negu = jnp.asarray(np.concatenate([tri, tri], axis=0), dtype=BF16)
    reps = LANES // RET_DK
    xc = x.reshape(s, d)
    for l in range(depth):
        head_gains = jnp.stack([jnp.tile(g[l], reps) for g in
                                (ret_q_norm, ret_k_norm, sb_q_norm, sb_k_norm)])
        rq, rk, rv, rg, sq, sk, sv, ga, gb = _in_proj(
            xc, norm_attn[l][None, :], w_in[l].astype(BF16), cos_t, sin_t, head_gains, gmat)
        ret = _retention(rq, rk, rv, rg, ret_group_norm[l][None, :], ret_consts)
        sb = _stickbreak(sq, sk, sv, negu)
        x2, hnt, s1t, s2t = _out_proj(
            ret, sb, ga, gb, xc, w_branch_ret[l].astype(BF16), w_branch_sb[l].astype(BF16),
            w_out[l].astype(BF16), norm_ffn[l][None, :], peer_w_q[l].astype(BF16),
            peer_sub_keys_1[l].astype(BF16), peer_sub_keys_2[l].astype(BF16))
        rank2, cnt, e1, e2 = _peer_sel(s1t, s2t)
        xc = _peer_ffn(hnt, peer_u[l].astype(BF16), peer_v[l].T.astype(BF16), rank2, cnt, e1, e2,
                       x2)
    return xc.reshape(b, s, d)
```

```python
import functools

import jax
import jax.numpy as jnp
import numpy as np
from jax import lax
from jax.experimental import pallas as pl
from jax.experimental.pallas import tpu as pltpu

F32 = jnp.float32
BF16 = jnp.bfloat16

EPS = 1e-6
ROPE_BASE = 10000.0

RET_HEADS = 8
RET_DK = 64
RET_DV = 128
RET_CHUNK = 128
SB_HEADS = 8
SB_DH = 64
PEER_HEADS = 8
N_KEYS = 128
PEER_TOPK = 16
PEER_DKEY = 256

LANES = 128
SUBLANES = 8
HEADS_PER_BLOCK = 2
VMEM_LIMIT = 56 * 1024 * 1024

IN_TILE = 512
RET_TILE = 1024
SB_TILE = 256
SEL_TILE = 512
FFN_TOKENS = 512
FFN_EXPERTS = 1024

NEG_INF = float("-inf")
MASKED_LOG = -1e30
SB_ZERO_WEIGHT_CARRY = 105.0


def _dot(a, b):
    return jnp.dot(a, b, preferred_element_type=F32)


def _dot_nt(a, b):
    return lax.dot_general(a, b, (((1,), (1,)), ((), ())), preferred_element_type=F32)


def _split_dot(a, b):
    hi = a.astype(BF16)
    lo = (a - hi.astype(F32)).astype(BF16)
    return _dot(hi, b) + _dot(lo, b)


def _sigmoid(x):
    return 1.0 / (1.0 + jnp.exp(-x))


def _in_proj_kernel(x_ref, g_ref, w_ref, cos_ref, sin_ref, hg_ref, gmat_ref,
                    rq_ref, rk_ref, rv_ref, rg_ref, sq_ref, sk_ref, sv_ref, ga_ref, gb_ref):
    xf = x_ref[...]
    ms = jnp.mean(xf * xf, axis=-1, keepdims=True)
    xn = (xf * lax.rsqrt(ms + EPS) * g_ref[...]).astype(BF16)
    gmat = gmat_ref[...]
    cos = cos_ref[...]
    sin = sin_ref[...]
    lane = lax.broadcasted_iota(jnp.int32, cos.shape, 1)
    first_half = (lane % RET_DK) < (RET_DK // 2)

    def head_norm(t, gain):
        return t * lax.rsqrt(_split_dot(t * t, gmat) + EPS) * gain

    def rope(y):
        partner = jnp.where(first_half, pltpu.roll(y, LANES - RET_DK // 2, 1),
                            pltpu.roll(y, RET_DK // 2, 1))
        return y * cos + partner * sin

    col = 0

    def proj(width):
        nonlocal col
        out = _dot(xn, w_ref[:, col:col + width])
        col += width
        return out

    qk = RET_HEADS * RET_DK
    t = proj(qk)
    for c in range(qk // LANES):
        blk = t[:, c * LANES:(c + 1) * LANES]
        rq_ref[:, c * LANES:(c + 1) * LANES] = rope(head_norm(blk, hg_ref[0:1, :])).astype(BF16)
    t = proj(qk)
    for c in range(qk // LANES):
        blk = t[:, c * LANES:(c + 1) * LANES]
        rk_ref[:, c * LANES:(c + 1) * LANES] = (
            rope(head_norm(blk, hg_ref[1:2, :])) * (RET_DK ** -0.5)).astype(BF16)
    rv_ref[...] = proj(RET_HEADS * RET_DV).astype(BF16)
    t = proj(RET_HEADS * RET_DV)
    rg_ref[...] = (t * _sigmoid(t)).astype(BF16)
    sw = SB_HEADS * SB_DH
    t = proj(sw)
    for c in range(sw // LANES):
        blk = t[:, c * LANES:(c + 1) * LANES]
        sq_ref[:, c * LANES:(c + 1) * LANES] = (
            head_norm(blk, hg_ref[2:3, :]) * (SB_DH ** -0.5)).astype(BF16)
    t = proj(sw)
    for c in range(sw // LANES):
        blk = t[:, c * LANES:(c + 1) * LANES]
        sk_ref[:, c * LANES:(c + 1) * LANES] = head_norm(blk, hg_ref[3:4, :]).astype(BF16)
    sv_ref[...] = proj(sw).astype(BF16)
    d_model = ga_ref.shape[1]
    ga_ref[...] = _sigmoid(proj(d_model)).astype(BF16)
    gb_ref[...] = _sigmoid(proj(d_model)).astype(BF16)


def _in_proj(x2d, norm_g, w_in, cos_t, sin_t, head_gains, gmat):
    s, d = x2d.shape
    width = w_in.shape[1]
    qk = RET_HEADS * RET_DK
    rv = RET_HEADS * RET_DV
    sw = SB_HEADS * SB_DH
    tm = IN_TILE
    row = lambda w: pl.BlockSpec((tm, w), lambda i: (i, 0))
    full = lambda a: pl.BlockSpec(a.shape, lambda i: (0,) * a.ndim)
    out_widths = (qk, qk, rv, rv, sw, sw, sw, d, d)
    return pl.pallas_call(
        _in_proj_kernel,
        grid=(s // tm,),
        in_specs=[row(d), full(norm_g), full(w_in), row(LANES), row(LANES), full(head_gains),
                  full(gmat)],
        out_specs=[row(w) for w in out_widths],
        out_shape=[jax.ShapeDtypeStruct((s, w), BF16) for w in out_widths],
        compiler_params=pltpu.CompilerParams(dimension_semantics=("arbitrary",),
                                             vmem_limit_bytes=VMEM_LIMIT),
        name="in_proj",
    )(x2d, norm_g, w_in, cos_t, sin_t, head_gains, gmat)


def _retention_kernel(q_ref, k_ref, v_ref, gate_ref, gn_ref, dmask_ref, qdec_ref, kdec_ref,
                      cdec_ref, bd_ref, o_ref, state_ref):
    @pl.when(pl.program_id(1) == 0)
    def _():
        state_ref[...] = jnp.zeros_like(state_ref)

    c = RET_CHUNK
    lane = lax.broadcasted_iota(jnp.int32, (c, LANES), 1)
    head0 = lane < RET_DK
    bd = bd_ref[...]
    qdec = qdec_ref[...]
    kdec = kdec_ref[...]
    cdec = cdec_ref[...]
    gn = gn_ref[...]

    def chunk(ci, carry):
        r0 = pl.multiple_of(ci * c, c)
        q = q_ref[pl.ds(r0, c), :]
        k = k_ref[pl.ds(r0, c), :]
        v = v_ref[pl.ds(r0, c), :]
        state = state_ref[...]
        zero = jnp.zeros_like(q)
        q_heads = (jnp.where(head0, q, zero), jnp.where(head0, zero, q))
        cross = _dot(q, state.astype(BF16)) * qdec
        outs = []
        for h in range(HEADS_PER_BLOCK):
            scores = _dot_nt(q_heads[h], k) * dmask_ref[h]
            y = _dot(scores.astype(BF16), v[:, h * RET_DV:(h + 1) * RET_DV])
            y = y + cross[:, h * RET_DV:(h + 1) * RET_DV]
            mu = jnp.mean(y, axis=-1, keepdims=True)
            yc = y - mu
            var = jnp.mean(yc * yc, axis=-1, keepdims=True)
            outs.append(yc * lax.rsqrt(var + EPS) * gn[:, h * RET_DV:(h + 1) * RET_DV])
        out = jnp.concatenate(outs, axis=1) * gate_ref[pl.ds(r0, c), :].astype(F32)
        o_ref[pl.ds(r0, c), :] = out.astype(o_ref.dtype)
        kd_t = (k.astype(F32) * kdec).T.astype(BF16)
        state_ref[...] = state * cdec + _dot(kd_t, v) * bd
        return carry

    lax.fori_loop(0, q_ref.shape[0] // c, chunk, 0)


def _retention(rq, rk, rv, gate, gn_gain, consts):
    s = rq.shape[0]
    dmask, qdec, kdec, cdec, bd = consts
    t = RET_TILE
    pairs = RET_HEADS // HEADS_PER_BLOCK
    vw = HEADS_PER_BLOCK * RET_DV
    return pl.pallas_call(
        _retention_kernel,
        grid=(pairs, s // t),
        in_specs=[
            pl.BlockSpec((t, LANES), lambda p, i: (i, p)),
            pl.BlockSpec((t, LANES), lambda p, i: (i, p)),
            pl.BlockSpec((t, vw), lambda p, i: (i, p)),
            pl.BlockSpec((t, vw), lambda p, i: (i, p)),
            pl.BlockSpec((1, vw), lambda p, i: (0, p)),
            pl.BlockSpec((HEADS_PER_BLOCK, RET_CHUNK, RET_CHUNK), lambda p, i: (p, 0, 0)),
            pl.BlockSpec((RET_CHUNK, vw), lambda p, i: (0, p)),
            pl.BlockSpec((RET_CHUNK, LANES), lambda p, i: (0, p)),
            pl.BlockSpec((1, vw), lambda p, i: (0, p)),
            pl.BlockSpec(bd.shape, lambda p, i: (0, 0)),
        ],
        out_specs=pl.BlockSpec((t, vw), lambda p, i: (i, p)),
        out_shape=jax.ShapeDtypeStruct((s, RET_HEADS * RET_DV), BF16),
        scratch_shapes=[pltpu.VMEM((LANES, vw), F32)],
        compiler_params=pltpu.CompilerParams(dimension_semantics=("arbitrary", "arbitrary"),
                                             vmem_limit_bytes=VMEM_LIMIT),
        name="retention",
    )(rq, rk, rv, gate, gn_gain, dmask, qdec, kdec, cdec, bd)


def _retention_consts():
    c = RET_CHUNK
    h = jnp.arange(RET_HEADS, dtype=F32)
    log_g = jnp.log1p(-jnp.exp2(-5.0 - h))
    i = jnp.arange(c, dtype=F32)
    diff = i[:, None] - i[None, :]
    dmask = jnp.where(diff >= 0, jnp.exp(log_g[:, None, None] * jnp.maximum(diff, 0.0)), 0.0)
    q_dec = jnp.exp(log_g[None, :] * (i[:, None] + 1.0))
    k_dec = jnp.exp(log_g[None, :] * (c - 1.0 - i[:, None]))
    c_dec = jnp.exp(log_g * c)
    qdec = jnp.repeat(q_dec, RET_DV, axis=1)
    kdec = jnp.repeat(k_dec, RET_DK, axis=1)
    cdec = jnp.repeat(c_dec, RET_DV)[None, :]
    rows = np.arange(LANES)[:, None] // RET_DK
    cols = np.arange(HEADS_PER_BLOCK * RET_DV)[None, :] // RET_DV
    bd = jnp.asarray((rows == cols).astype(np.float32))
    return dmask, qdec, kdec, cdec, bd


def _stickbreak_kernel(q_ref, k_ref, v_ref, negu_ref, o_ref, z_ref, hl_ref, rs_ref, w_ref):
    t = SB_TILE
    i = pl.program_id(1)
    n_tiles = i + 1
    q = q_ref[...]
    lane = lax.broadcasted_iota(jnp.int32, (t, LANES), 1)
    head0 = lane < SB_DH
    zero = jnp.zeros_like(q)
    q_both = jnp.concatenate([jnp.where(head0, q, zero), jnp.where(head0, zero, q)], axis=0)

    def key_rows(m):
        j = jnp.clip(i - m, 0, i)
        return pl.ds(pl.multiple_of(j * t, t), t)

    def slot3(m):
        return (m + 3) % 3

    def slot2(m):
        return (m + 2) % 2

    def scores(m):
        z_ref[slot3(m)] = _dot_nt(q_both, k_ref[key_rows(m), :])

    def softplus_terms(m, diagonal):
        z = z_ref[slot3(m)]
        sp = jnp.maximum(z, 0.0) + jnp.log(1.0 + jnp.exp(-jnp.abs(z)))
        if diagonal:
            row = lax.broadcasted_iota(jnp.int32, (2 * t, t), 0) % t
            col = lax.broadcasted_iota(jnp.int32, (2 * t, t), 1)
            sp = jnp.where(col < row, sp, 0.0)
            z_ref[slot3(m)] = jnp.where(col < row, z, MASKED_LOG)
        hi = sp.astype(BF16)
        hl_ref[slot2(m), :, pl.ds(0, t)] = hi
        hl_ref[slot2(m), :, pl.ds(t, t)] = (sp - hi.astype(F32)).astype(BF16)
        rs_ref[slot2(m)] = jnp.sum(sp, axis=-1, keepdims=True)

    def suffix_sums(m):
        return _dot(hl_ref[slot2(m)], negu_ref[...])

    def weights(m, loc, c, live=None):
        c_eff = c if live is None else jnp.where(live, c, -MASKED_LOG)
        w_ref[slot2(m)] = jnp.exp(z_ref[slot3(m)] + loc - c_eff).astype(BF16)
        return c + rs_ref[slot2(m)]

    def weighted_values(m, acc):
        vt = v_ref[key_rows(m), :]
        vzero = jnp.zeros_like(vt)
        w = w_ref[slot2(m)]
        return (acc + _dot(w[:t], jnp.where(head0, vt, vzero))
                + _dot(w[t:], jnp.where(head0, vzero, vt)))

    w_ref[1] = jnp.zeros(w_ref.shape[1:], BF16)
    scores(0)
    scores(1)
    softplus_terms(0, True)
    n_run = jnp.maximum(n_tiles, 2)

    def body(carry):
        n, c, acc, _ = carry
        smallest_carry = jnp.min(c)
        loc = suffix_sums(n - 2)
        acc = weighted_values(n - 3, acc)
        c = weights(n - 2, loc, c)
        softplus_terms(n - 1, False)
        scores(n)
        return n + 1, c, acc, smallest_carry

    def unfinished(carry):
        n, _, _, smallest_carry = carry
        return jnp.logical_and(n < n_run, smallest_carry < SB_ZERO_WEIGHT_CARRY)

    init = (jnp.int32(2), jnp.zeros((2 * t, 1), F32), jnp.zeros((t, LANES), F32), jnp.float32(0.0))
    n_end, c, acc, _ = lax.while_loop(unfinished, body, init)
    o_ref[...] = acc.astype(o_ref.dtype)

    @pl.when(n_end == n_run)
    def _():
        loc = suffix_sums(n_run - 2)
        acc1 = weighted_values(n_run - 3, acc)
        c1 = weights(n_run - 2, loc, c)
        softplus_terms(n_run - 1, False)
        loc = suffix_sums(n_run - 1)
        acc1 = weighted_values(n_run - 2, acc1)
        weights(n_run - 1, loc, c1, live=n_tiles > 1)
        acc1 = weighted_values(n_run - 1, acc1)
        o_ref[...] = acc1.astype(o_ref.dtype)


def _stickbreak(sq, sk, sv, negu):
    s = sq.shape[0]
    t = SB_TILE
    pairs = SB_HEADS // HEADS_PER_BLOCK
    return pl.pallas_call(
        _stickbreak_kernel,
        grid=(pairs, s // t),
        in_specs=[
            pl.BlockSpec((t, LANES), lambda p, i: (i, p)),
            pl.BlockSpec((s, LANES), lambda p, i: (0, p)),
            pl.BlockSpec((s, LANES), lambda p, i: (0, p)),
            pl.BlockSpec((2 * t, t), lambda p, i: (0, 0)),
        ],
        out_specs=pl.BlockSpec((t, LANES), lambda p, i: (i, p)),
        out_shape=jax.ShapeDtypeStruct((s, SB_HEADS * SB_DH), BF16),
        scratch_shapes=[
            pltpu.VMEM((3, 2 * t, t), F32),
            pltpu.VMEM((2, 2 * t, 2 * t), BF16),
            pltpu.VMEM((2, 2 * t, 1), F32),
            pltpu.VMEM((2, 2 * t, t), BF16),
        ],
        compiler_params=pltpu.CompilerParams(dimension_semantics=("arbitrary", "arbitrary"),
                                             vmem_limit_bytes=VMEM_LIMIT),
        name="stickbrk",
    )(sq, sk, sv, negu)


def _out_proj_kernel(ret_ref, sb_ref, ga_ref, gb_ref, x_ref, wa_ref, wb_ref, wo_ref, g_ref,
                     wq_ref, k1_ref, k2_ref, x2_ref, hnt_ref, s1_ref, s2_ref):
    ya = _dot(ret_ref[...], wa_ref[...])
    yb = _dot(sb_ref[...], wb_ref[...])
    mixed = ga_ref[...].astype(F32) * ya + gb_ref[...].astype(F32) * yb
    x2 = x_ref[...] + _dot(mixed.astype(BF16), wo_ref[...])
    x2_ref[...] = x2
    ms = jnp.mean(x2 * x2, axis=-1, keepdims=True)
    hn = x2 * lax.rsqrt(ms + EPS) * g_ref[...]
    hnt_ref[...] = hn.T.astype(BF16)
    qp = _dot(hn.astype(BF16), wq_ref[...]).astype(BF16)
    k1 = k1_ref[...]
    k2 = k2_ref[...]
    half = PEER_DKEY // 2
    for h in range(PEER_HEADS):
        s1_ref[h] = _dot_nt(k1, qp[:, h * PEER_DKEY:h * PEER_DKEY + half])
        s2_ref[h] = _dot_nt(k2, qp[:, h * PEER_DKEY + half:(h + 1) * PEER_DKEY])


def _out_proj(ret, sb, ga, gb, x2d, wa, wb, wo, norm_g, wq, k1, k2):
    s, d = x2d.shape
    tm = IN_TILE
    row = lambda a: pl.BlockSpec((tm, a.shape[1]), lambda i: (i, 0))
    full = lambda a: pl.BlockSpec(a.shape, lambda i: (0,) * a.ndim)
    score_spec = pl.BlockSpec((PEER_HEADS, N_KEYS, tm), lambda i: (0, 0, i))
    score_shape = jax.ShapeDtypeStruct((PEER_HEADS, N_KEYS, s), F32)
    return pl.pallas_call(
        _out_proj_kernel,
        grid=(s // tm,),
        in_specs=[row(ret), row(sb), row(ga), row(gb), row(x2d), full(wa), full(wb), full(wo),
                  full(norm_g), full(wq), full(k1), full(k2)],
        out_specs=[pl.BlockSpec((tm, d), lambda i: (i, 0)), pl.BlockSpec((d, tm), lambda i: (0, i)),
                   score_spec, score_spec],
        out_shape=[jax.ShapeDtypeStruct((s, d), F32), jax.ShapeDtypeStruct((d, s), BF16),
                   score_shape, score_shape],
        compiler_params=pltpu.CompilerParams(dimension_semantics=("arbitrary",),
                                             vmem_limit_bytes=VMEM_LIMIT),
        name="out_proj",
    )(ret, sb, ga, gb, x2d, wa, wb, wo, norm_g, wq, k1, k2)


_CAND_GROUPS = [(a, PEER_TOPK // (a + 1)) for a in range(PEER_TOPK // 2)]


def _extract_top(x, count, row_index):
    rank = jnp.full(x.shape, float(PEER_TOPK), F32)
    vals = []
    big = float(x.shape[0])
    for r in range(count):
        m = jnp.max(x, axis=0, keepdims=True)
        first = jnp.min(jnp.where(x == m, row_index, big), axis=0, keepdims=True)
        hit = row_index == first
        rank = jnp.where(hit, float(r), rank)
        x = jnp.where(hit, NEG_INF, x)
        vals.append(m)
    return vals, rank


def _stack_rows(rows, row_index):
    out = jnp.zeros(row_index.shape, F32)
    for r, v in enumerate(rows):
        out = jnp.where(row_index == float(r), v, out)
    return out


def _peer_sel_kernel(s1_ref, s2_ref, rank2_ref, cnt_ref, e1_ref, e2_ref):
    k = PEER_TOPK
    key_index = lax.broadcasted_iota(jnp.int32, (N_KEYS, LANES), 0).astype(F32)
    idx16 = lax.broadcasted_iota(jnp.int32, (k, LANES), 0).astype(F32)
    idx8 = idx16[:8]
    n_cand_rows = 8 * (len(_CAND_GROUPS) + 2)
    cand_index = lax.broadcasted_iota(jnp.int32, (n_cand_rows, LANES), 0).astype(F32)
    chunks = s1_ref.shape[2] // LANES

    def unit(uidx, carry):
        h = uidx // chunks
        l0 = pl.multiple_of((uidx % chunks) * LANES, LANES)
        x1 = s1_ref[h, :, pl.ds(l0, LANES)]
        x2 = s2_ref[h, :, pl.ds(l0, LANES)]
        v1, rank1 = _extract_top(x1, k, key_index)
        v2, rank2 = _extract_top(x2, k, key_index)
        v1s = _stack_rows(v1, idx16)
        v2s = _stack_rows(v2, idx16)
        groups = [v1[0] + v2s]
        for a, n in _CAND_GROUPS[1:]:
            groups.append(jnp.where(idx8 < float(n), v1[a] + v2s[:8], NEG_INF))
        groups.append(v1s[8:] + v2[0])
        cand = jnp.concatenate(groups, axis=0)
        _, crank = _extract_top(cand, k, cand_index)
        sel = crank < float(k)
        top = v1[0] + v2[0]
        z = jnp.sum(jnp.where(sel, jnp.exp(cand - top), 0.0), axis=0, keepdims=True)
        self32 = sel.astype(F32)
        counts = [jnp.sum(self32[0:16], axis=0, keepdims=True)]
        for g in range(1, len(_CAND_GROUPS)):
            counts.append(jnp.sum(self32[8 + 8 * g:16 + 8 * g], axis=0, keepdims=True))
        tail = 8 + 8 * len(_CAND_GROUPS)
        for r in range(8):
            counts.append(self32[tail + r:tail + r + 1])
        cnt = jnp.zeros((N_KEYS, LANES), F32)
        for a in range(k):
            cnt = jnp.where(rank1 == float(a), counts[a], cnt)
        rank2_ref[h, :, pl.ds(l0, LANES)] = rank2.astype(BF16)
        cnt_ref[h, :, pl.ds(l0, LANES)] = cnt
        e1_ref[h, :, pl.ds(l0, LANES)] = jnp.exp(x1 - v1[0])
        e2_ref[h, :, pl.ds(l0, LANES)] = (jnp.exp(x2 - v2[0]) * (1.0 / z)).astype(BF16)
        return carry

    lax.fori_loop(0, PEER_HEADS * chunks, unit, 0)


def _peer_sel(s1t, s2t):
    s = s1t.shape[2]
    t = SEL_TILE
    spec = pl.BlockSpec((PEER_HEADS, N_KEYS, t), lambda i: (0, 0, i))
    shape = jax.ShapeDtypeStruct(s1t.shape, F32)
    return pl.pallas_call(
        _peer_sel_kernel,
        grid=(s // t,),
        in_specs=[spec, spec],
        out_specs=[spec] * 4,
        out_shape=[jax.ShapeDtypeStruct(s1t.shape, BF16), shape, shape,
                   jax.ShapeDtypeStruct(s1t.shape, BF16)],
        compiler_params=pltpu.CompilerParams(dimension_semantics=("arbitrary",),
                                             vmem_limit_bytes=VMEM_LIMIT),
        name="peer_sel",
    )(s1t, s2t)


def _gelu_tanh(x):
    return 0.5 * x * (1.0 + jnp.tanh(0.7978845608028654 * (x + 0.044715 * (x * x * x))))


def _peer_ffn_kernel(hnt_ref, u_ref, vt_ref, rank2_ref, cnt_ref, e1_ref, e2_ref, x2_ref, o_ref,
                     yt_ref, at_ref):
    e = pl.program_id(1)

    @pl.when(e == 0)
    def _():
        yt_ref[...] = jnp.zeros_like(yt_ref)

    rows = u_ref.shape[0] // N_KEYS
    assert rows == SUBLANES
    chunks = hnt_ref.shape[1] // LANES
    ht = _dot(u_ref[...], hnt_ref[...])
    i0 = pl.multiple_of(e * rows, rows)
    packed_rows = 2 * SUBLANES
    zero = jnp.zeros((N_KEYS, LANES), BF16)

    def row_bf16(ref, h, r, ls):
        row = ref[h, pl.ds(i0, rows), ls][r:r + 1]
        packed = jnp.broadcast_to(row, (packed_rows, LANES)).astype(BF16)
        return jnp.tile(packed, (N_KEYS // packed_rows, 1))

    for r in range(rows):
        for c in range(chunks):
            ls = slice(c * LANES, (c + 1) * LANES)
            gate = zero
            for h in range(PEER_HEADS):
                selected = rank2_ref[h, :, ls] < row_bf16(cnt_ref, h, r, ls)
                gate = gate + jnp.where(selected, e2_ref[h, :, ls], zero) * row_bf16(e1_ref, h, r, ls)
            hblk = ht[r * N_KEYS:(r + 1) * N_KEYS, ls].astype(BF16)
            at_ref[r * N_KEYS:(r + 1) * N_KEYS, ls] = _gelu_tanh(hblk) * gate
    yt_ref[...] += _dot(vt_ref[...], at_ref[...])

    @pl.when(e == pl.num_programs(1) - 1)
    def _():
        o_ref[...] = x2_ref[...] + yt_ref[...].T


def _peer_ffn(hnt, u_bf, vt_bf, rank2, cnt, e1, e2, x2):
    s, d = x2.shape
    n_exp = u_bf.shape[0]
    tt = FFN_TOKENS
    te = FFN_EXPERTS
    sel_spec = pl.BlockSpec((PEER_HEADS, N_KEYS, tt), lambda t, e: (0, 0, t))
    return pl.pallas_call(
        _peer_ffn_kernel,
        grid=(s // tt, n_exp // te),
        in_specs=[
            pl.BlockSpec((d, tt), lambda t, e: (0, t)),
            pl.BlockSpec((te, d), lambda t, e: (e, 0)),
            pl.BlockSpec((d, te), lambda t, e: (0, e)),
            sel_spec, sel_spec, sel_spec, sel_spec,
            pl.BlockSpec((tt, d), lambda t, e: (t, 0)),
        ],
        out_specs=pl.BlockSpec((tt, d), lambda t, e: (t, 0)),
        out_shape=jax.ShapeDtypeStruct((s, d), F32),
        scratch_shapes=[pltpu.VMEM((d, tt), F32), pltpu.VMEM((te, tt), BF16)],
        compiler_params=pltpu.CompilerParams(dimension_semantics=("arbitrary", "arbitrary"),
                                             vmem_limit_bytes=VMEM_LIMIT),
        name="peer_ffn",
    )(hnt, u_bf, vt_bf, rank2, cnt, e1, e2, x2)


def _rope_tables(s):
    half = RET_DK // 2
    pos = jnp.arange(s, dtype=F32)
    freqs = ROPE_BASE ** (-jnp.arange(half, dtype=F32) / half)
    ang = pos[:, None] * freqs[None, :]
    cos = jnp.cos(ang)
    sin = jnp.sin(ang)
    reps = LANES // RET_DK
    cos_t = jnp.tile(jnp.concatenate([cos, cos], axis=-1), (1, reps))
    sin_t = jnp.tile(jnp.concatenate([-sin, sin], axis=-1), (1, reps))
    return cos_t, sin_t


def kernel(x, norm_attn, w_in, ret_q_norm, ret_k_norm, ret_group_norm, sb_q_norm, sb_k_norm,
           w_branch_ret, w_branch_sb, w_out, norm_ffn, peer_w_q, peer_sub_keys_1,
           peer_sub_keys_2, peer_u, peer_v):
    b, s, d = x.shape
    assert b == 1 and s % RET_TILE == 0 and s % IN_TILE == 0 and s % FFN_TOKENS == 0
    depth = w_in.shape[0]
    cos_t, sin_t = _rope_tables(s)
    ret_consts = _retention_consts()
    grp = np.arange(LANES) // RET_DK
    gmat = jnp.asarray((grp[:, None] == grp[None, :]).astype(np.float32) / RET_DK, dtype=BF16)
    kk = np.arange(SB_TILE)
    tri = -(kk[:, None] >= kk[None, :]).astype(np.float32)
    negu = jnp.asarray(np.concatenate([tri, tri], axis=0), dtype=BF16)
    reps = LANES // RET_DK
    xc = x.reshape(s, d)
    for l in range(depth):
        head_gains = jnp.stack([jnp.tile(g[l], reps) for g in
                                (ret_q_norm, ret_k_norm, sb_q_norm, sb_k_norm)])
        rq, rk, rv, rg, sq, sk, sv, ga, gb = _in_proj(
            xc, norm_attn[l][None, :], w_in[l].astype(BF16), cos_t, sin_t, head_gains, gmat)
        ret = _retention(rq, rk, rv, rg, ret_group_norm[l][None, :], ret_consts)
        sb = _stickbreak(sq, sk, sv, negu)
        x2, hnt, s1t, s2t = _out_proj(
            ret, sb, ga, gb, xc, w_branch_ret[l].astype(BF16), w_branch_sb[l].astype(BF16),
            w_out[l].astype(BF16), norm_ffn[l][None, :], peer_w_q[l].astype(BF16),
            peer_sub_keys_1[l].astype(BF16), peer_sub_keys_2[l].astype(BF16))
        rank2, cnt, e1, e2 = _peer_sel(s1t, s2t)
        xc = _peer_ffn(hnt, peer_u[l].astype(BF16), peer_v[l].T.astype(BF16), rank2, cnt, e1, e2,
                       x2)
    return xc.reshape(b, s, d)
```
